```python
import math
import jax, jax.numpy as jnp
from jax import lax
import numpy as np

D_MODEL = 1024
BATCH = 2
SEQ = 8192
DEPTH = 2
DEC_BATCH = 128
DEC_SEQ = 1
PAST_LEN = 8192
PAGE_SIZE = 128

D_A = D_MODEL // 2
H_A = 4
DK_A = D_A // H_A
DV_A = D_A // H_A
MLSTM_CHUNK = 128
D_B = D_MODEL // 2
S5_GROUP = 16
S5_STATE = 64
G_B = D_B // S5_GROUP
D_C = D_MODEL // 2
HD_C = 64
H_C = D_C // HD_C
H_KV = 2
REP_C = H_C // H_KV
WINDOW = 128
ATTN_BLOCK = 128
ROPE_THETA = 10000.0
D_FF = 2816
CONV_W = 3
P_DIM = 256
N_BRANCH = 3
EPS = 1e-6

kernel_name = "hybrid_mlstm_s5_swa_decoder_step"


def _in_sizes():
    return [D_A, D_A, D_A, D_A, H_A, H_A, D_B, D_C, H_KV * HD_C, H_KV * HD_C, N_BRANCH * D_MODEL]


def split_cols(z):
    idx = [int(v) for v in np.cumsum(_in_sizes())[:-1]]
    return jnp.split(z, idx, axis=-1)


def rmsnorm(x, g):
    xf = x.astype(jnp.float32)
    r = lax.rsqrt(jnp.mean(xf * xf, axis=-1, keepdims=True) + EPS)
    return (xf * r * g.astype(jnp.float32)).astype(x.dtype)


def rope(x, pos):
    half = x.shape[-1] // 2
    inv = ROPE_THETA ** (-jnp.arange(half, dtype=jnp.float32) / half)
    ang = pos.astype(jnp.float32)[:, None] * inv[None, :]
    cos, sin = jnp.cos(ang)[:, None, :], jnp.sin(ang)[:, None, :]
    xf = x.astype(jnp.float32)
    x1, x2 = xf[..., :half], xf[..., half:]
    return jnp.concatenate([x1 * cos - x2 * sin, x2 * cos + x1 * sin], axis=-1).astype(x.dtype)


def heads_first(a, n_heads, hd):
    b, l, _ = a.shape
    return a.reshape(b, l, n_heads, hd).transpose(0, 2, 1, 3)


def mlstm_chunk(carry, inp):
    c_prev, n_prev, m_prev = carry
    q, k, v, ig, lf = inp
    L = q.shape[2]
    b = jnp.cumsum(lf, axis=-1)
    causal = jnp.tril(jnp.ones((L, L), dtype=bool))
    dmat = jnp.where(causal, b[..., :, None] - b[..., None, :] + ig[..., None, :], -jnp.inf)
    inter = b + m_prev[..., None]
    m_row = jnp.maximum(inter, jnp.max(dmat, axis=-1))
    w_intra = jnp.exp(dmat - m_row[..., None])
    w_inter = jnp.exp(inter - m_row)
    s = jnp.einsum("bhtk,bhsk->bhts", q, k) * w_intra
    num = w_inter[..., None] * jnp.einsum("bhtk,bhkv->bhtv", q, c_prev) + jnp.einsum("bhts,bhsv->bhtv", s, v)
    den = w_inter * jnp.einsum("bhtk,bhk->bht", q, n_prev) + jnp.sum(s, axis=-1)
    h = num / jnp.maximum(jnp.abs(den), jnp.exp(-m_row))[..., None]
    b_last = b[..., -1]
    g_s = b_last[..., None] - b + ig
    m_new = jnp.maximum(b_last + m_prev, jnp.max(g_s, axis=-1))
    w_old = jnp.exp(b_last + m_prev - m_new)
    w_s = jnp.exp(g_s - m_new[..., None])
    c_new = w_old[..., None, None] * c_prev + jnp.einsum("bhs,bhsk,bhsv->bhkv", w_s, k, v)
    n_new = w_old[..., None] * n_prev + jnp.einsum("bhs,bhsk->bhk", w_s, k)
    return (c_new, n_new, m_new), h


def mlstm_prompt(q, k, v, ig, lf, carry0):
    b, h, t, _ = q.shape
    nc = t // MLSTM_CHUNK

    def chunks(a):
        return jnp.moveaxis(a.reshape((b, h, nc, MLSTM_CHUNK) + a.shape[3:]), 2, 0)

    carry, hs = lax.scan(mlstm_chunk, carry0, (chunks(q), chunks(k), chunks(v), chunks(ig), chunks(lf)))
    return jnp.moveaxis(hs, 0, 2).reshape(b, h, t, hs.shape[-1]), carry


def _cmul_scan(e1, e2):
    a1r, a1i, b1r, b1i = e1
    a2r, a2i, b2r, b2i = e2
    return (a2r * a1r - a2i * a1i, a2r * a1i + a2i * a1r,
            a2r * b1r - a2i * b1i + b2r, a2r * b1i + a2i * b1r + b2i)


def s5_branch(u, h0_re, h0_im, a_re, a_im, log_dt, b_re, b_im, c_re, c_im, d_skip, w_glu):
    bsz, L, _ = u.shape
    f32 = jnp.float32
    uf = u.astype(f32).reshape(bsz, L, G_B, S5_GROUP)
    ar, ai = a_re.astype(f32), a_im.astype(f32)
    dt = jnp.exp(log_dt.astype(f32))[:, None]
    mag = jnp.exp(ar * dt)
    lr, li = mag * jnp.cos(ai * dt), mag * jnp.sin(ai * dt)
    inv_den = 1.0 / (ar * ar + ai * ai)
    cr = ((lr - 1.0) * ar + li * ai) * inv_den
    ci = (li * ar - (lr - 1.0) * ai) * inv_den
    br, bi = b_re.astype(f32), b_im.astype(f32)
    bbr = cr[..., None] * br - ci[..., None] * bi
    bbi = cr[..., None] * bi + ci[..., None] * br
    xr = jnp.einsum("blgi,gpi->blgp", uf, bbr)
    xi = jnp.einsum("blgi,gpi->blgp", uf, bbi)
    h0r, h0i = h0_re.astype(f32), h0_im.astype(f32)
    xr = xr.at[:, 0].add(lr * h0r - li * h0i)
    xi = xi.at[:, 0].add(lr * h0i + li * h0r)
    lam_r = jnp.broadcast_to(lr, xr.shape)
    lam_i = jnp.broadcast_to(li, xi.shape)
    _, _, sr, si = lax.associative_scan(_cmul_scan, (lam_r, lam_i, xr, xi), axis=1)
    y = jnp.einsum("blgp,gip->blgi", sr, c_re.astype(f32)) - jnp.einsum("blgp,gip->blgi", si, c_im.astype(f32))
    y = y.reshape(bsz, L, D_B) + d_skip.astype(f32) * u.astype(f32)
    y = jax.nn.gelu(y)
    y = y * jax.nn.sigmoid(y @ w_glu.astype(f32))
    return y.astype(u.dtype), sr[:, -1], si[:, -1]


def attend_with_sink(q, k, v, mask, sinks):
    s = jnp.einsum("...tgrd,...sgd->...grts", q, k).astype(jnp.float32) * (HD_C ** -0.5)
    s = jnp.where(mask, s, -jnp.inf)
    sk = sinks.astype(jnp.float32).reshape(H_KV, REP_C)[:, :, None]
    mx = jnp.maximum(jnp.max(s, axis=-1), sk)
    p = jnp.exp(s - mx[..., None])
    denom = jnp.sum(p, axis=-1) + jnp.exp(sk - mx)
    attn = (p / denom[..., None]).astype(v.dtype)
    return jnp.einsum("...grts,...sgd->...tgrd", attn, v)


def swa_prompt(q, k, v, sinks):
    b, t = q.shape[0], q.shape[1]
    nb = t // ATTN_BLOCK
    qb = q.reshape(b, nb, ATTN_BLOCK, H_KV, REP_C, HD_C)
    kb = k.reshape(b, nb, ATTN_BLOCK, H_KV, HD_C)
    vb = v.reshape(b, nb, ATTN_BLOCK, H_KV, HD_C)

    def with_prev(a):
        prev = jnp.concatenate([jnp.zeros_like(a[:, :1]), a[:, :-1]], axis=1)
        return jnp.concatenate([prev, a], axis=2)

    kk, vv = with_prev(kb), with_prev(vb)
    tq = jnp.arange(ATTN_BLOCK)[:, None]
    sk = jnp.arange(2 * ATTN_BLOCK)[None, :]
    diff = tq + ATTN_BLOCK - sk
    blk = jnp.arange(nb)[:, None, None]
    mask = (diff >= 0) & (diff < WINDOW) & (blk * ATTN_BLOCK + sk - ATTN_BLOCK >= 0)
    out = attend_with_sink(qb, kk, vv, mask[None, :, None, None], sinks)
    return out.reshape(b, t, D_C)


def swa_sample(q, k, v, cache_k, cache_v, sinks):
    b, L = q.shape[0], q.shape[1]
    w_buf = cache_k.shape[1]
    kk = jnp.concatenate([cache_k.astype(k.dtype), k], axis=1)
    vv = jnp.concatenate([cache_v.astype(v.dtype), v], axis=1)
    qpos = PAST_LEN + jnp.arange(L)
    kpos = PAST_LEN - w_buf + jnp.arange(w_buf + L)
    diff = qpos[:, None] - kpos[None, :]
    mask = (diff >= 0) & (diff < WINDOW)
    out = attend_with_sink(q.reshape(b, L, H_KV, REP_C, HD_C), kk, vv, mask[None, None, None], sinks)
    return out.reshape(b, L, D_C)


def conv_ffn(h, prev_rows, w_up, conv_w, conv_b, w_down):
    L = h.shape[1]
    up = h @ w_up
    g, val = jnp.split(up, 2, axis=-1)
    gext = jnp.concatenate([prev_rows.astype(g.dtype), g], axis=1)
    gc = conv_b
    for j in range(CONV_W):
        gc = gc + conv_w[j] * gext[:, j:j + L]
    return (jax.nn.silu(gc) * val) @ w_down, g


def run_layer(x, pe, pos, lw, st, is_prompt):
    f32 = jnp.float32
    bsz, L, _ = x.shape
    h = rmsnorm(x, lw["norm_mix"])
    z = h @ lw["w_in"] + lw["b_in"]
    qa, ka, va, oa, ia, fa, ub, qc, kc, vc, gates = split_cols(z)

    q_a = heads_first(qa, H_A, DK_A).astype(f32)
    k_a = heads_first(ka, H_A, DK_A).astype(f32) * (DK_A ** -0.5)
    v_a = heads_first(va, H_A, DV_A).astype(f32)
    ig = jnp.swapaxes(ia.astype(f32), 1, 2)
    lf = jax.nn.log_sigmoid(jnp.swapaxes(fa.astype(f32), 1, 2))
    carry0 = (st["mlstm_c"].astype(f32), st["mlstm_n"].astype(f32), st["mlstm_m"].astype(f32))
    if is_prompt:
        h_a, (c_new, n_new, m_new) = mlstm_prompt(q_a, k_a, v_a, ig, lf, carry0)
    else:
        (c_new, n_new, m_new), h_a = mlstm_chunk(carry0, (q_a, k_a, v_a, ig, lf))
    h_a = rmsnorm(h_a, lw["norm_h_a"])
    h_a = jnp.swapaxes(h_a, 1, 2).reshape(bsz, L, D_A).astype(x.dtype) * jax.nn.sigmoid(oa)

    y_b, s_re, s_im = s5_branch(ub, st["s5_re"], st["s5_im"], lw["s5_a_re"], lw["s5_a_im"], lw["s5_log_dt"],
                                lw["s5_b_re"], lw["s5_b_im"], lw["s5_c_re"], lw["s5_c_im"], lw["s5_d"], lw["s5_w_glu"])

    q_c = rope(rmsnorm(qc.reshape(bsz, L, H_C, HD_C), lw["q_norm"]), pos)
    k_c = rope(rmsnorm(kc.reshape(bsz, L, H_KV, HD_C), lw["k_norm"]), pos)
    v_c = vc.reshape(bsz, L, H_KV, HD_C)
    if is_prompt:
        y_c = swa_prompt(q_c, k_c, v_c, lw["sinks"])
        w_keep = min(WINDOW, L)
        k_rows, v_rows = k_c[:, L - w_keep:], v_c[:, L - w_keep:]
    else:
        y_c = swa_sample(q_c, k_c, v_c, st["swa_k"], st["swa_v"], lw["sinks"])
        k_rows, v_rows = k_c, v_c

    g_a, g_b, g_c = jnp.split(gates, N_BRANCH, axis=-1)
    mix = (jax.nn.sigmoid(g_a) * (h_a @ lw["w_br_a"])
           + jax.nn.sigmoid(g_b) * (y_b @ lw["w_br_b"])
           + jax.nn.sigmoid(g_c) * (y_c @ lw["w_br_c"]))
    x = x + mix @ lw["w_out"]

    f_out, g_up = conv_ffn(rmsnorm(x, lw["norm_ffn"]), st["ffn_conv"], lw["w_up"], lw["conv_w"],
                           lw["conv_b"], lw["w_down"])
    x = x + f_out
    conv_rows = g_up[:, L - (CONV_W - 1):] if is_prompt else g_up

    gate = jax.nn.sigmoid(rmsnorm(x, lw["norm_ple"]) @ lw["w_pg"])
    x = x + gate * (pe @ lw["w_pe"])
    return x, (c_new, n_new, m_new, s_re, s_im, k_rows, v_rows, conv_rows)


def setup_inputs(seed: int = 0) -> dict:
    key = jax.random.key(seed)
    keys = jax.random.split(key, 48)
    ctr = [0]

    def nk():
        k = keys[ctr[0]]
        ctr[0] += 1
        return k

    def nrm(shape, scale):
        return jax.random.normal(nk(), shape, jnp.float32) * scale

    w_buf = min(WINDOW, PAST_LEN)
    n_in = sum(_in_sizes())
    f_off = 4 * D_A + H_A
    b_in = nrm((DEPTH, n_in), 0.02).at[:, f_off:f_off + H_A].add(jnp.linspace(3.0, 6.0, H_A))
    return {
        "x_prompt": nrm((BATCH, SEQ, D_MODEL), 1.0),
        "x_sample": nrm((DEC_BATCH, DEC_SEQ, D_MODEL), 1.0),
        "p_prompt": nrm((DEPTH, BATCH, SEQ, P_DIM), 1.0),
        "p_sample": nrm((DEPTH, DEC_BATCH, DEC_SEQ, P_DIM), 1.0),
        "state_mlstm_c": nrm((DEPTH, DEC_BATCH, H_A, DK_A, DV_A), 0.5),
        "state_mlstm_n": nrm((DEPTH, DEC_BATCH, H_A, DK_A), 0.5),
        "state_mlstm_m": nrm((DEPTH, DEC_BATCH, H_A), 1.0),
        "state_s5_re": nrm((DEPTH, DEC_BATCH, G_B, S5_STATE), 0.1),
        "state_s5_im": nrm((DEPTH, DEC_BATCH, G_B, S5_STATE), 0.1),
        "cache_swa_k": nrm((DEPTH, DEC_BATCH, w_buf, H_KV, HD_C), 1.0),
        "cache_swa_v": nrm((DEPTH, DEC_BATCH, w_buf, H_KV, HD_C), 1.0),
        "state_ffn_conv": nrm((DEPTH, DEC_BATCH, CONV_W - 1, D_FF), 0.5),
        "norm_mix": 1.0 + nrm((DEPTH, D_MODEL), 0.01),
        "w_in": nrm((DEPTH, D_MODEL, n_in), D_MODEL ** -0.5),
        "b_in": b_in,
        "norm_h_a": 1.0 + nrm((DEPTH, DV_A), 0.01),
        "s5_a_re": -0.5 + nrm((DEPTH, G_B, S5_STATE), 0.01),
        "s5_a_im": jnp.pi * jnp.arange(S5_STATE, dtype=jnp.float32) + nrm((DEPTH, G_B, S5_STATE), 0.01),
        "s5_log_dt": jax.random.uniform(nk(), (DEPTH, G_B), jnp.float32, math.log(1e-3), math.log(1e-1)),
        "s5_b_re": nrm((DEPTH, G_B, S5_STATE, S5_GROUP), (2.0 * S5_GROUP) ** -0.5),
        "s5_b_im": nrm((DEPTH, G_B, S5_STATE, S5_GROUP), (2.0 * S5_GROUP) ** -0.5),
        "s5_c_re": nrm((DEPTH, G_B, S5_GROUP, S5_STATE), (2.0 * S5_STATE) ** -0.5),
        "s5_c_im": nrm((DEPTH, G_B, S5_GROUP, S5_STATE), (2.0 * S5_STATE) ** -0.5),
        "s5_d": nrm((DEPTH, D_B), 1.0),
        "s5_w_glu": nrm((DEPTH, D_B, D_B), D_B ** -0.5),
        "q_norm": 1.0 + nrm((DEPTH, HD_C), 0.01),
        "k_norm": 1.0 + nrm((DEPTH, HD_C), 0.01),
        "attn_sinks": nrm((DEPTH, H_C), 0.5),
        "w_br_a": nrm((DEPTH, D_A, D_MODEL), D_A ** -0.5),
        "w_br_b": nrm((DEPTH, D_B, D_MODEL), D_B ** -0.5),
        "w_br_c": nrm((DEPTH, D_C, D_MODEL), D_C ** -0.5),
        "w_out": nrm((DEPTH, D_MODEL, D_MODEL), D_MODEL ** -0.5),
        "norm_ffn": 1.0 + nrm((DEPTH, D_MODEL), 0.01),
        "w_up": nrm((DEPTH, D_MODEL, 2 * D_FF), D_MODEL ** -0.5),
        "conv_w": nrm((DEPTH, CONV_W, D_FF), CONV_W ** -0.5),
        "conv_b": nrm((DEPTH, D_FF), 0.02),
        "w_down": nrm((DEPTH, D_FF, D_MODEL), D_FF ** -0.5),
        "norm_ple": 1.0 + nrm((DEPTH, D_MODEL), 0.01),
        "w_pg": nrm((DEPTH, D_MODEL, D_MODEL), D_MODEL ** -0.5),
        "w_pe": nrm((DEPTH, P_DIM, D_MODEL), P_DIM ** -0.5),
    }


def reference(x_prompt, x_sample, p_prompt, p_sample, state_mlstm_c, state_mlstm_n, state_mlstm_m,
              state_s5_re, state_s5_im, cache_swa_k, cache_swa_v, state_ffn_conv,
              norm_mix, w_in, b_in, norm_h_a, s5_a_re, s5_a_im, s5_log_dt, s5_b_re, s5_b_im,
              s5_c_re, s5_c_im, s5_d, s5_w_glu, q_norm, k_norm, attn_sinks, w_br_a, w_br_b, w_br_c,
              w_out, norm_ffn, w_up, conv_w, conv_b, w_down, norm_ple, w_pg, w_pe):
    f32 = jnp.float32
    bp, lp = x_prompt.shape[0], x_prompt.shape[1]
    ls = x_sample.shape[1]
    pos_p = jnp.arange(lp)
    pos_s = PAST_LEN + jnp.arange(ls)
    xp, xs = x_prompt, x_sample
    new_p, new_s = [], []
    for i in range(DEPTH):
        lw = dict(norm_mix=norm_mix[i], w_in=w_in[i], b_in=b_in[i], norm_h_a=norm_h_a[i],
                  s5_a_re=s5_a_re[i], s5_a_im=s5_a_im[i], s5_log_dt=s5_log_dt[i],
                  s5_b_re=s5_b_re[i], s5_b_im=s5_b_im[i], s5_c_re=s5_c_re[i], s5_c_im=s5_c_im[i],
                  s5_d=s5_d[i], s5_w_glu=s5_w_glu[i], q_norm=q_norm[i], k_norm=k_norm[i],
                  sinks=attn_sinks[i], w_br_a=w_br_a[i], w_br_b=w_br_b[i], w_br_c=w_br_c[i],
                  w_out=w_out[i], norm_ffn=norm_ffn[i], w_up=w_up[i], conv_w=conv_w[i],
                  conv_b=conv_b[i], w_down=w_down[i], norm_ple=norm_ple[i], w_pg=w_pg[i], w_pe=w_pe[i])
        st_p = dict(mlstm_c=jnp.zeros((bp, H_A, DK_A, DV_A), f32),
                    mlstm_n=jnp.zeros((bp, H_A, DK_A), f32),
                    mlstm_m=jnp.zeros((bp, H_A), f32),
                    s5_re=jnp.zeros((bp, G_B, S5_STATE), f32),
                    s5_im=jnp.zeros((bp, G_B, S5_STATE), f32),
                    ffn_conv=jnp.zeros((bp, CONV_W - 1, D_FF), x_prompt.dtype))
        xp, out_p = run_layer(xp, p_prompt[i], pos_p, lw, st_p, True)
        st_s = dict(mlstm_c=state_mlstm_c[i], mlstm_n=state_mlstm_n[i], mlstm_m=state_mlstm_m[i],
                    s5_re=state_s5_re[i], s5_im=state_s5_im[i], swa_k=cache_swa_k[i],
                    swa_v=cache_swa_v[i], ffn_conv=state_ffn_conv[i])
        xs, out_s = run_layer(xs, p_sample[i], pos_s, lw, st_s, False)
        new_p.append(out_p)
        new_s.append(out_s)
    c_p, n_p, m_p, re_p, im_p, k_p, v_p, conv_p = [jnp.stack(a) for a in zip(*new_p)]
    c_s, n_s, m_s, re_s, im_s, k_s, v_s, conv_s = [jnp.stack(a) for a in zip(*new_s)]
    return (xp, xs, c_p, n_p, m_p, re_p, im_p, k_p, v_p, conv_p,
            c_s, n_s, m_s, re_s, im_s, k_s, v_s, conv_s)
```

```python
import jax
import jax.numpy as jnp
import numpy as np
from jax import lax
from jax.experimental import pallas as pl
from jax.experimental.pallas import tpu as pltpu

F32 = jnp.float32
BF16 = jnp.bfloat16

D_MODEL = 1024
PAST_LEN = 8192
D_A = 512
H_A = 4
DK_A = 128
DV_A = 128
MLSTM_CHUNK = 128
D_B = 512
S5_GROUP = 16
S5_STATE = 64
G_B = 32
D_C = 512
HD_C = 64
H_C = 8
H_KV = 2
REP_C = 4
WINDOW = 128
ATTN_BLOCK = 128
ROPE_THETA = 10000.0
D_FF = 2816
CONV_W = 3
EPS = 1e-6
VMEM_LIMIT = 48 * 1024 * 1024

Z_QA, Z_KA, Z_VA, Z_OA, Z_UB, Z_QC, Z_GATES, Z_KC, Z_VC, Z_IF = (
    0, 512, 1024, 1536, 2048, 2560, 3072, 6144, 6272, 6400)
NZ = 6528
O_IA, O_UB, O_QC, O_KC, O_VC, O_GATES, N_IN = 2048, 2056, 2568, 3080, 3208, 3336, 6408


def _cparams(sem):
    return pltpu.CompilerParams(dimension_semantics=sem, vmem_limit_bytes=VMEM_LIMIT)


def _sigmoid(x):
    return 1.0 / (1.0 + jnp.exp(-x))


def _inproj_kernel(x_ref, g_ref, w_ref, b_ref, z_ref):
    x = x_ref[...]
    r = lax.rsqrt(jnp.mean(x * x, axis=-1, keepdims=True) + EPS)
    h = (x * r * g_ref[...]).astype(BF16)
    z_ref[...] = jnp.dot(h, w_ref[...], preferred_element_type=F32) + b_ref[...]


def _inproj(x2d, g, w, b, tm):
    rows = x2d.shape[0]
    tn = NZ // 3
    return pl.pallas_call(
        _inproj_kernel,
        grid=(NZ // tn, rows // tm),
        in_specs=[
            pl.BlockSpec((tm, D_MODEL), lambda j, i: (i, 0)),
            pl.BlockSpec((1, D_MODEL), lambda j, i: (0, 0)),
            pl.BlockSpec((D_MODEL, tn), lambda j, i: (0, j)),
            pl.BlockSpec((1, tn), lambda j, i: (0, j)),
        ],
        out_specs=pl.BlockSpec((tm, tn), lambda j, i: (i, j)),
        out_shape=jax.ShapeDtypeStruct((rows, NZ), F32),
        compiler_params=_cparams(("arbitrary", "arbitrary")),
        name="inproj",
    )(x2d, g, w, b)


def _merge_kernel(x_ref, ga_ref, gb_ref, gc_ref, ya_ref, yb_ref, yc_ref,
                  wa_ref, wb_ref, wc_ref, wo_ref, o_ref):
    mix = (_sigmoid(ga_ref[...]) * jnp.dot(ya_ref[...], wa_ref[...], preferred_element_type=F32)
           + _sigmoid(gb_ref[...]) * jnp.dot(yb_ref[...], wb_ref[...], preferred_element_type=F32)
           + _sigmoid(gc_ref[...]) * jnp.dot(yc_ref[...], wc_ref[...], preferred_element_type=F32))
    o_ref[...] = x_ref[...] + jnp.dot(mix.astype(BF16), wo_ref[...], preferred_element_type=F32)


def _merge(x2d, z, ya, yb, yc, wa, wb, wc, wo, tm):
    rows = x2d.shape[0]
    gblk = Z_GATES // D_MODEL
    row_spec = lambda w, cb=0: pl.BlockSpec((tm, w), lambda i: (i, cb))
    w_spec = lambda k: pl.BlockSpec((k, D_MODEL), lambda i: (0, 0))
    return pl.pallas_call(
        _merge_kernel,
        grid=(rows // tm,),
        in_specs=[row_spec(D_MODEL), row_spec(D_MODEL, gblk), row_spec(D_MODEL, gblk + 1),
                  row_spec(D_MODEL, gblk + 2), row_spec(D_A), row_spec(D_B), row_spec(D_C),
                  w_spec(D_A), w_spec(D_B), w_spec(D_C), w_spec(D_MODEL)],
        out_specs=row_spec(D_MODEL),
        out_shape=jax.ShapeDtypeStruct((rows, D_MODEL), F32),
        compiler_params=_cparams(("arbitrary",)),
        name="merge",
    )(x2d, z, z, z, ya, yb, yc, wa, wb, wc, wo)


def _rmsnorm(x, g):
    xf = x.astype(F32)
    r = lax.rsqrt(jnp.mean(xf * xf, axis=-1, keepdims=True) + EPS)
    return (xf * r * g.astype(F32)).astype(x.dtype)


def _rope(x, pos):
    half = x.shape[-1] // 2
    inv = ROPE_THETA ** (-jnp.arange(half, dtype=F32) / half)
    ang = pos.astype(F32)[:, None] * inv[None, :]
    cos, sin = jnp.cos(ang)[:, None, :], jnp.sin(ang)[:, None, :]
    xf = x.astype(F32)
    x1, x2 = xf[..., :half], xf[..., half:]
    return jnp.concatenate([x1 * cos - x2 * sin, x2 * cos + x1 * sin], axis=-1).astype(x.dtype)


def _heads_first(a, n_heads, hd):
    b, l, _ = a.shape
    return a.reshape(b, l, n_heads, hd).transpose(0, 2, 1, 3)


def _mlstm_chunk(carry, inp):
    c_prev, n_prev, m_prev = carry
    q, k, v, ig, lf = inp
    L = q.shape[2]
    b = jnp.cumsum(lf, axis=-1)
    causal = jnp.tril(jnp.ones((L, L), dtype=bool))
    dmat = jnp.where(causal, b[..., :, None] - b[..., None, :] + ig[..., None, :], -jnp.inf)
    inter = b + m_prev[..., None]
    m_row = jnp.maximum(inter, jnp.max(dmat, axis=-1))
    w_intra = jnp.exp(dmat - m_row[..., None])
    w_inter = jnp.exp(inter - m_row)
    s = jnp.einsum("bhtk,bhsk->bhts", q, k) * w_intra
    num = w_inter[..., None] * jnp.einsum("bhtk,bhkv->bhtv", q, c_prev) + jnp.einsum("bhts,bhsv->bhtv", s, v)
    den = w_inter * jnp.einsum("bhtk,bhk->bht", q, n_prev) + jnp.sum(s, axis=-1)
    h = num / jnp.maximum(jnp.abs(den), jnp.exp(-m_row))[..., None]
    b_last = b[..., -1]
    g_s = b_last[..., None] - b + ig
    m_new = jnp.maximum(b_last + m_prev, jnp.max(g_s, axis=-1))
    w_old = jnp.exp(b_last + m_prev - m_new)
    w_s = jnp.exp(g_s - m_new[..., None])
    c_new = w_old[..., None, None] * c_prev + jnp.einsum("bhs,bhsk,bhsv->bhkv", w_s, k, v)
    n_new = w_old[..., None] * n_prev + jnp.einsum("bhs,bhsk->bhk", w_s, k)
    return (c_new, n_new, m_new), h


def _mlstm_prompt(q, k, v, ig, lf, carry0):
    b, h, t, _ = q.shape
    nc = t // MLSTM_CHUNK

    def chunks(a):
        return jnp.moveaxis(a.reshape((b, h, nc, MLSTM_CHUNK) + a.shape[3:]), 2, 0)

    carry, hs = lax.scan(_mlstm_chunk, carry0, (chunks(q), chunks(k), chunks(v), chunks(ig), chunks(lf)))
    return jnp.moveaxis(hs, 0, 2).reshape(b, h, t, hs.shape[-1]), carry


def _cmul_scan(e1, e2):
    a1r, a1i, b1r, b1i = e1
    a2r, a2i, b2r, b2i = e2
    return (a2r * a1r - a2i * a1i, a2r * a1i + a2i * a1r,
            a2r * b1r - a2i * b1i + b2r, a2r * b1i + a2i * b1r + b2i)


def _s5_branch(u, h0_re, h0_im, a_re, a_im, log_dt, b_re, b_im, c_re, c_im, d_skip, w_glu):
    bsz, L, _ = u.shape
    uf = u.astype(F32).reshape(bsz, L, G_B, S5_GROUP)
    ar, ai = a_re.astype(F32), a_im.astype(F32)
    dt = jnp.exp(log_dt.astype(F32))[:, None]
    mag = jnp.exp(ar * dt)
    lr, li = mag * jnp.cos(ai * dt), mag * jnp.sin(ai * dt)
    inv_den = 1.0 / (ar * ar + ai * ai)
    cr = ((lr - 1.0) * ar + li * ai) * inv_den
    ci = (li * ar - (lr - 1.0) * ai) * inv_den
    br, bi = b_re.astype(F32), b_im.astype(F32)
    bbr = cr[..., None] * br - ci[..., None] * bi
    bbi = cr[..., None] * bi + ci[..., None] * br
    xr = jnp.einsum("blgi,gpi->blgp", uf, bbr)
    xi = jnp.einsum("blgi,gpi->blgp", uf, bbi)
    h0r, h0i = h0_re.astype(F32), h0_im.astype(F32)
    xr = xr.at[:, 0].add(lr * h0r - li * h0i)
    xi = xi.at[:, 0].add(lr * h0i + li * h0r)
    lam_r = jnp.broadcast_to(lr, xr.shape)
    lam_i = jnp.broadcast_to(li, xi.shape)
    _, _, sr, si = lax.associative_scan(_cmul_scan, (lam_r, lam_i, xr, xi), axis=1)
    y = jnp.einsum("blgp,gip->blgi", sr, c_re.astype(F32)) - jnp.einsum("blgp,gip->blgi", si, c_im.astype(F32))
    y = y.reshape(bsz, L, D_B) + d_skip.astype(F32) * u.astype(F32)
    y = jax.nn.gelu(y)
    y = y * jax.nn.sigmoid(y @ w_glu.astype(F32))
    return y.astype(u.dtype), sr[:, -1], si[:, -1]


def _attend_with_sink(q, k, v, mask, sinks):
    s = jnp.einsum("...tgrd,...sgd->...grts", q, k).astype(F32) * (HD_C ** -0.5)
    s = jnp.where(mask, s, -jnp.inf)
    sk = sinks.astype(F32).reshape(H_KV, REP_C)[:, :, None]
    mx = jnp.maximum(jnp.max(s, axis=-1), sk)
    p = jnp.exp(s - mx[..., None])
    denom = jnp.sum(p, axis=-1) + jnp.exp(sk - mx)
    attn = (p / denom[..., None]).astype(v.dtype)
    return jnp.einsum("...grts,...sgd->...tgrd", attn, v)


def _swa_prompt(q, k, v, sinks):
    b, t = q.shape[0], q.shape[1]
    nb = t // ATTN_BLOCK
    qb = q.reshape(b, nb, ATTN_BLOCK, H_KV, REP_C, HD_C)
    kb = k.reshape(b, nb, ATTN_BLOCK, H_KV, HD_C)
    vb = v.reshape(b, nb, ATTN_BLOCK, H_KV, HD_C)

    def with_prev(a):
        prev = jnp.concatenate([jnp.zeros_like(a[:, :1]), a[:, :-1]], axis=1)
        return jnp.concatenate([prev, a], axis=2)

    kk, vv = with_prev(kb), with_prev(vb)
    tq = jnp.arange(ATTN_BLOCK)[:, None]
    sk = jnp.arange(2 * ATTN_BLOCK)[None, :]
    diff = tq + ATTN_BLOCK - sk
    blk = jnp.arange(nb)[:, None, None]
    mask = (diff >= 0) & (diff < WINDOW) & (blk * ATTN_BLOCK + sk - ATTN_BLOCK >= 0)
    out = _attend_with_sink(qb, kk, vv, mask[None, :, None, None], sinks)
    return out.reshape(b, t, D_C)


def _swa_sample(q, k, v, cache_k, cache_v, sinks):
    b, L = q.shape[0], q.shape[1]
    w_buf = cache_k.shape[1]
    kk = jnp.concatenate([cache_k.astype(k.dtype), k], axis=1)
    vv = jnp.concatenate([cache_v.astype(v.dtype), v], axis=1)
    qpos = PAST_LEN + jnp.arange(L)
    kpos = PAST_LEN - w_buf + jnp.arange(w_buf + L)
    diff = qpos[:, None] - kpos[None, :]
    mask = (diff >= 0) & (diff < WINDOW)
    out = _attend_with_sink(q.reshape(b, L, H_KV, REP_C, HD_C), kk, vv, mask[None, None, None], sinks)
    return out.reshape(b, L, D_C)


def _conv_ffn(h, prev_rows, w_up, conv_w, conv_b, w_down):
    L = h.shape[1]
    up = h @ w_up
    g, val = jnp.split(up, 2, axis=-1)
    gext = jnp.concatenate([prev_rows.astype(g.dtype), g], axis=1)
    gc = conv_b
    for j in range(CONV_W):
        gc = gc + conv_w[j] * gext[:, j:j + L]
    return (jax.nn.silu(gc) * val) @ w_down, g


def _permute_in(w, b):
    pad_w = jnp.zeros((D_MODEL, NZ - Z_IF - 2 * H_A), F32)
    w_perm = jnp.concatenate(
        [w[:, :O_IA], w[:, O_UB:O_QC], w[:, O_QC:O_KC], w[:, O_GATES:N_IN], w[:, O_KC:O_VC],
         w[:, O_VC:O_GATES], w[:, O_IA:O_UB], pad_w], axis=1).astype(BF16)
    b_perm = jnp.concatenate(
        [b[:O_IA], b[O_UB:O_QC], b[O_QC:O_KC], b[O_GATES:N_IN], b[O_KC:O_VC],
         b[O_VC:O_GATES], b[O_IA:O_UB], pad_w[0]])[None, :]
    return w_perm, b_perm


def _run_layer(x, pe, pos, lw, st, is_prompt):
    bsz, L, _ = x.shape
    rows = bsz * L
    tm = min(512, rows)
    x2d = x.reshape(rows, D_MODEL)
    w_perm, b_perm = _permute_in(lw["w_in"], lw["b_in"])
    z = _inproj(x2d, lw["norm_mix"][None, :], w_perm, b_perm, tm)
    seg = lambda off, width: z[:, off:off + width].reshape(bsz, L, width)
    qa, ka, va, oa = seg(Z_QA, D_A), seg(Z_KA, D_A), seg(Z_VA, D_A), seg(Z_OA, D_A)
    ub, qc = seg(Z_UB, D_B), seg(Z_QC, D_C)
    kc, vc = seg(Z_KC, H_KV * HD_C), seg(Z_VC, H_KV * HD_C)
    ia, fa = seg(Z_IF, H_A), seg(Z_IF + H_A, H_A)

    q_a = _heads_first(qa, H_A, DK_A)
    k_a = _heads_first(ka, H_A, DK_A) * (DK_A ** -0.5)
    v_a = _heads_first(va, H_A, DV_A)
    ig = jnp.swapaxes(ia, 1, 2)
    lf = jax.nn.log_sigmoid(jnp.swapaxes(fa, 1, 2))
    carry0 = (st["mlstm_c"].astype(F32), st["mlstm_n"].astype(F32), st["mlstm_m"].astype(F32))
    if is_prompt:
        h_a, (c_new, n_new, m_new) = _mlstm_prompt(q_a, k_a, v_a, ig, lf, carry0)
    else:
        (c_new, n_new, m_new), h_a = _mlstm_chunk(carry0, (q_a, k_a, v_a, ig, lf))
    h_a = _rmsnorm(h_a, lw["norm_h_a"])
    h_a = jnp.swapaxes(h_a, 1, 2).reshape(bsz, L, D_A) * jax.nn.sigmoid(oa)

    y_b, s_re, s_im = _s5_branch(ub, st["s5_re"], st["s5_im"], lw["s5_a_re"], lw["s5_a_im"], lw["s5_log_dt"],
                                 lw["s5_b_re"], lw["s5_b_im"], lw["s5_c_re"], lw["s5_c_im"], lw["s5_d"],
                                 lw["s5_w_glu"])

    q_c = _rope(_rmsnorm(qc.reshape(bsz, L, H_C, HD_C), lw["q_norm"]), pos)
    k_c = _rope(_rmsnorm(kc.reshape(bsz, L, H_KV, HD_C), lw["k_norm"]), pos)
    v_c = vc.reshape(bsz, L, H_KV, HD_C)
    if is_prompt:
        y_c = _swa_prompt(q_c, k_c, v_c, lw["sinks"])
        w_keep = min(WINDOW, L)
        k_rows, v_rows = k_c[:, L - w_keep:], v_c[:, L - w_keep:]
    else:
        y_c = _swa_sample(q_c, k_c, v_c, st["swa_k"], st["swa_v"], lw["sinks"])
        k_rows, v_rows = k_c, v_c

    flat = lambda a, w: a.reshape(rows, w).astype(BF16)
    x2d = _merge(x2d, z, flat(h_a, D_A), flat(y_b, D_B), flat(y_c, D_C),
                 lw["w_br_a"].astype(BF16), lw["w_br_b"].astype(BF16), lw["w_br_c"].astype(BF16),
                 lw["w_out"].astype(BF16), tm)
    x = x2d.reshape(bsz, L, D_MODEL)

    f_out, g_up = _conv_ffn(_rmsnorm(x, lw["norm_ffn"]), st["ffn_conv"], lw["w_up"], lw["conv_w"],
                            lw["conv_b"], lw["w_down"])
    x = x + f_out
    conv_rows = g_up[:, L - (CONV_W - 1):] if is_prompt else g_up
    gate = jax.nn.sigmoid(_rmsnorm(x, lw["norm_ple"]) @ lw["w_pg"])
    x = x + gate * (pe @ lw["w_pe"])
    return x, (c_new, n_new, m_new, s_re, s_im, k_rows, v_rows, conv_rows)


def kernel(x_prompt, x_sample, p_prompt, p_sample, state_mlstm_c, state_mlstm_n, state_mlstm_m, state_s5_re, state_s5_im, cache_swa_k, cache_swa_v, state_ffn_conv, norm_mix, w_in, b_in, norm_h_a, s5_a_re, s5_a_im, s5_log_dt, s5_b_re, s5_b_im, s5_c_re, s5_c_im, s5_d, s5_w_glu, q_norm, k_norm, attn_sinks, w_br_a, w_br_b, w_br_c, w_out, norm_ffn, w_up, conv_w, conv_b, w_down, norm_ple, w_pg, w_pe):
    depth = w_in.shape[0]
    bp, lp = x_prompt.shape[0], x_prompt.shape[1]
    ls = x_sample.shape[1]
    pos_p = jnp.arange(lp)
    pos_s = PAST_LEN + jnp.arange(ls)
    xp, xs = x_prompt, x_sample
    new_p, new_s = [], []
    for i in range(depth):
        lw = dict(norm_mix=norm_mix[i], w_in=w_in[i], b_in=b_in[i], norm_h_a=norm_h_a[i],
                  s5_a_re=s5_a_re[i], s5_a_im=s5_a_im[i], s5_log_dt=s5_log_dt[i],
                  s5_b_re=s5_b_re[i], s5_b_im=s5_b_im[i], s5_c_re=s5_c_re[i], s5_c_im=s5_c_im[i],
                  s5_d=s5_d[i], s5_w_glu=s5_w_glu[i], q_norm=q_norm[i], k_norm=k_norm[i],
                  sinks=attn_sinks[i], w_br_a=w_br_a[i], w_br_b=w_br_b[i], w_br_c=w_br_c[i],
                  w_out=w_out[i], norm_ffn=norm_ffn[i], w_up=w_up[i], conv_w=conv_w[i],
                  conv_b=conv_b[i], w_down=w_down[i], norm_ple=norm_ple[i], w_pg=w_pg[i], w_pe=w_pe[i])
        st_p = dict(mlstm_c=jnp.zeros((bp, H_A, DK_A, DV_A), F32),
                    mlstm_n=jnp.zeros((bp, H_A, DK_A), F32),
                    mlstm_m=jnp.zeros((bp, H_A), F32),
                    s5_re=jnp.zeros((bp, G_B, S5_STATE), F32),
                    s5_im=jnp.zeros((bp, G_B, S5_STATE), F32),
                    ffn_conv=jnp.zeros((bp, CONV_W - 1, D_FF), x_prompt.dtype))
        xp, out_p = _run_layer(xp, p_prompt[i], pos_p, lw, st_p, True)
        st_s = dict(mlstm_c=state_mlstm_c[i], mlstm_n=state_mlstm_n[i], mlstm_m=state_mlstm_m[i],
                    s5_re=state_s5_re[i], s5_im=state_s5_im[i], swa_k=cache_swa_k[i],
                    swa_v=cache_swa_v[i], ffn_conv=state_ffn_conv[i])
        xs, out_s = _run_layer(xs, p_sample[i], pos_s, lw, st_s, False)
        new_p.append(out_p)
        new_s.append(out_s)
    outs_p = [jnp.stack(a) for a in zip(*new_p)]
    outs_s = [jnp.stack(a) for a in zip(*new_s)]
    return (xp, xs, *outs_p, *outs_s)
```
